```python
import math
import jax, jax.numpy as jnp
from jax import lax
import numpy as np

D_MODEL = 2048
BATCH = 4
SEQ = 2048
DEPTH = 4
DEC_BATCH = 8
DEC_SEQ = 4
PAST_LEN = 16384
PAGE_SIZE = 128

N_META = 16
D_CONV = 1024
CONV_W = 31
FOX_HEADS = 8
FOX_HD = 128
RET_HEADS = 8
RET_DK = 128
RET_DV = 128
D_FF = ((8 * D_MODEL + 3 * 256 - 1) // (3 * 256)) * 256
Q_BLOCK = 128
RET_CHUNK = 128
ROPE_BASE = 10000.0
LN_EPS = 1e-5
NEG_BIG = -1e30
ALPHA = (2.0 * DEPTH) ** 0.25
BETA = (8.0 * DEPTH) ** -0.25
D_FOX = FOX_HEADS * FOX_HD
D_RET_K = RET_HEADS * RET_DK
D_RET_V = RET_HEADS * RET_DV
SPLIT_SIZES = (D_CONV, D_CONV, D_FOX, D_FOX, D_FOX, FOX_HEADS, D_RET_K, D_RET_K, D_RET_V, D_RET_V, D_MODEL, D_MODEL, D_MODEL)
SPLIT_POINTS = tuple(int(s) for s in np.cumsum(SPLIT_SIZES)[:-1])
N_IN = int(sum(SPLIT_SIZES))

kernel_name = "hybrid_conv_fox_retention_decoder_step"


def layer_norm(x, g, b):
    xf = x.astype(jnp.float32)
    mu = jnp.mean(xf, -1, keepdims=True)
    var = jnp.mean(jnp.square(xf - mu), -1, keepdims=True)
    return ((xf - mu) * lax.rsqrt(var + LN_EPS)).astype(x.dtype) * g + b


def head_norm(o):
    mu = jnp.mean(o, -1, keepdims=True)
    var = jnp.mean(jnp.square(o - mu), -1, keepdims=True)
    return (o - mu) * lax.rsqrt(var + LN_EPS)


def rotary(x, pos):
    half = x.shape[-1] // 2
    inv = ROPE_BASE ** (-jnp.arange(half, dtype=jnp.float32) / half)
    ang = pos.astype(jnp.float32)[:, None] * inv[None, :]
    cos = jnp.cos(ang)[None, :, None, :]
    sin = jnp.sin(ang)[None, :, None, :]
    xf = x.astype(jnp.float32)
    x1, x2 = xf[..., :half], xf[..., half:]
    return jnp.concatenate([x1 * cos - x2 * sin, x1 * sin + x2 * cos], -1).astype(x.dtype)


def ret_log_gamma():
    return jnp.log(1.0 - jnp.power(2.0, -5.0 - jnp.arange(RET_HEADS, dtype=jnp.float32)))


def ret_chunk(S, q, k, v):
    C = q.shape[1]
    lg = ret_log_gamma()
    idx = jnp.arange(C, dtype=jnp.float32)
    diff = idx[:, None] - idx[None, :]
    dmask = jnp.where(diff >= 0, jnp.exp(lg[:, None, None] * jnp.maximum(diff, 0.0)), 0.0)
    qf, kf, vf = q.astype(jnp.float32), k.astype(jnp.float32), v.astype(jnp.float32)
    scores = jnp.einsum('bihd,bjhd->bhij', qf, kf) * dmask[None]
    inner = jnp.einsum('bhij,bjhe->bihe', scores, vf)
    cross = jnp.einsum('bihd,bhde->bihe', qf, S) * jnp.exp(lg[None, :] * (idx[:, None] + 1.0))[None, :, :, None]
    kd = kf * jnp.exp(lg[None, :] * (C - 1.0 - idx)[:, None])[None, :, :, None]
    S_new = S * jnp.exp(lg * C)[None, :, None, None] + jnp.einsum('bjhd,bjhe->bhde', kd, vf)
    return S_new, inner + cross


def retention_prompt(q, k, v):
    B, L = q.shape[0], q.shape[1]
    S0 = jnp.zeros((B, RET_HEADS, RET_DK, RET_DV), jnp.float32)
    S, o_meta = ret_chunk(S0, q[:, :N_META], k[:, :N_META], v[:, :N_META])
    n_chunks = (L - N_META) // RET_CHUNK

    def to_chunks(t):
        return t[:, N_META:].reshape(B, n_chunks, RET_CHUNK, t.shape[2], t.shape[3]).swapaxes(0, 1)

    def body(S, qkv):
        qc, kc, vc = qkv
        return ret_chunk(S, qc, kc, vc)

    S, o = lax.scan(body, S, (to_chunks(q), to_chunks(k), to_chunks(v)))
    o = o.swapaxes(0, 1).reshape(B, n_chunks * RET_CHUNK, RET_HEADS, RET_DV)
    return jnp.concatenate([o_meta, o], 1), S


def fox_attend(q, k, v, rq, rk, mask):
    qf = q.astype(jnp.float32) * (FOX_HD ** -0.5)
    s = jnp.einsum('bqhd,bkhd->bhqk', qf, k.astype(jnp.float32))
    s = s + (jnp.transpose(rk, (0, 2, 1))[:, :, None, :] - jnp.transpose(rq, (0, 2, 1))[:, :, :, None])
    s = jnp.where(mask[None, None], s, NEG_BIG)
    p = jax.nn.softmax(s, axis=-1)
    return jnp.einsum('bhqk,bkhd->bqhd', p, v.astype(jnp.float32)).astype(v.dtype)


def fox_prompt(q, k, v, lf):
    B, L = q.shape[0], q.shape[1]
    r = lax.cumsum(lf, axis=1, reverse=True) - lf
    kpos = jnp.arange(L)
    mpos = kpos[:N_META]
    o_meta = fox_attend(q[:, :N_META], k[:, :N_META], v[:, :N_META], r[:, :N_META], r[:, :N_META],
                        mpos[:, None] >= mpos[None, :])
    nb = (L - N_META) // Q_BLOCK
    qb = q[:, N_META:].reshape(B, nb, Q_BLOCK, FOX_HEADS, FOX_HD).swapaxes(0, 1)
    rb = r[:, N_META:].reshape(B, nb, Q_BLOCK, FOX_HEADS).swapaxes(0, 1)
    pb = kpos[N_META:].reshape(nb, Q_BLOCK)

    def blk(args):
        qc, rc, pc = args
        return fox_attend(qc, k, v, rc, r, pc[:, None] >= kpos[None, :])

    o = lax.map(blk, (qb, rb, pb)).swapaxes(0, 1).reshape(B, nb * Q_BLOCK, FOX_HEADS, FOX_HD)
    return jnp.concatenate([o_meta, o], 1)


def fox_sample(q, k, v, lf, k_past, v_past, lf_past):
    P, T = k_past.shape[1], q.shape[1]
    lf_all = jnp.concatenate([lf_past.astype(jnp.float32), lf], 1)
    r = lax.cumsum(lf_all, axis=1, reverse=True) - lf_all
    r_past = jnp.transpose(r[:, :P], (0, 2, 1))
    r_new = jnp.transpose(r[:, P:], (0, 2, 1))
    qf = q.astype(jnp.float32) * (FOX_HD ** -0.5)
    s_past = jnp.einsum('bqhd,bkhd->bhqk', qf, k_past.astype(jnp.float32)) + (r_past[:, :, None, :] - r_new[:, :, :, None])
    s_new = jnp.einsum('bqhd,bkhd->bhqk', qf, k.astype(jnp.float32)) + (r_new[:, :, None, :] - r_new[:, :, :, None])
    tpos = jnp.arange(T)
    s_new = jnp.where((tpos[:, None] >= tpos[None, :])[None, None], s_new, NEG_BIG)
    p = jax.nn.softmax(jnp.concatenate([s_past, s_new], -1), axis=-1)
    o = jnp.einsum('bhqk,bkhd->bqhd', p[..., :P], v_past.astype(jnp.float32)) + \
        jnp.einsum('bhqk,bkhd->bqhd', p[..., P:], v.astype(jnp.float32))
    return o.astype(v.dtype)


def dwconv(xpad, w, b):
    out = lax.conv_general_dilated(xpad, w[:, None, :], window_strides=(1,), padding='VALID',
                                   dimension_numbers=('NWC', 'WIO', 'NWC'), feature_group_count=w.shape[1])
    return out + b


def branch_inputs(u, w_in_l, b_f_l, pos):
    B, T, _ = u.shape
    z = u @ w_in_l
    (c_val, c_gate, f_q, f_k, f_v, f_f, r_q, r_k, r_v, r_g, g_a, g_b, g_c) = jnp.split(z, SPLIT_POINTS, axis=-1)
    glu = c_val * jax.nn.sigmoid(c_gate)
    fq = f_q.reshape(B, T, FOX_HEADS, FOX_HD)
    fk = f_k.reshape(B, T, FOX_HEADS, FOX_HD)
    fv = f_v.reshape(B, T, FOX_HEADS, FOX_HD)
    lf = jax.nn.log_sigmoid(f_f.astype(jnp.float32) + b_f_l.astype(jnp.float32))
    rq = rotary(r_q.reshape(B, T, RET_HEADS, RET_DK), pos)
    rk = rotary(r_k.reshape(B, T, RET_HEADS, RET_DK), pos) * (RET_DK ** -0.5)
    rv = r_v.reshape(B, T, RET_HEADS, RET_DV)
    return glu, fq, fk, fv, lf, rq, rk, rv, r_g, g_a, g_b, g_c


def conv_tail(cpad, conv_w_l, conv_b_l, ln_g, ln_b, w_proj):
    h = dwconv(cpad, conv_w_l, conv_b_l)
    h = jax.nn.silu(layer_norm(h, ln_g, ln_b))
    return h @ w_proj


def ret_tail(o, r_g, w_proj):
    B, T = o.shape[0], o.shape[1]
    y = head_norm(o).reshape(B, T, D_RET_V).astype(r_g.dtype)
    return (jax.nn.silu(r_g) * y) @ w_proj


def merge_and_ffn(x, y_conv, y_fox, y_ret, g_a, g_b, g_c, w_out_l, ln1_g_l, ln1_b_l, w_gate_l, w_up_l, w_down_l, ln2_g_l, ln2_b_l):
    mix = (jax.nn.sigmoid(g_a) * y_conv + jax.nn.sigmoid(g_b) * y_fox + jax.nn.sigmoid(g_c) * y_ret) @ w_out_l
    h = layer_norm(ALPHA * x + mix, ln1_g_l, ln1_b_l)
    f = (jax.nn.silu(h @ w_gate_l) * (h @ w_up_l)) @ w_down_l
    return layer_norm(ALPHA * h + f, ln2_g_l, ln2_b_l)


def setup_inputs(seed: int = 0) -> dict:
    key = jax.random.key(seed)
    ks = jax.random.split(key, 40)
    n_pages = PAST_LEN // PAGE_SIZE
    n_used = DEC_BATCH * n_pages
    n_phys = n_used + (n_used + 3) // 4

    def nrm(k, shape, scale):
        return jax.random.normal(k, shape, jnp.float32) * scale

    page_table = jax.random.permutation(ks[7], n_phys)[:n_used].reshape(DEC_BATCH, n_pages).astype(jnp.int32)
    return {
        "x_prompt": nrm(ks[0], (BATCH, SEQ, D_MODEL), 1.0),
        "x_sample": nrm(ks[1], (DEC_BATCH, DEC_SEQ, D_MODEL), 1.0),
        "cache_fox_k": nrm(ks[2], (DEPTH, n_phys, PAGE_SIZE, FOX_HEADS, FOX_HD), 1.0),
        "cache_fox_v": nrm(ks[3], (DEPTH, n_phys, PAGE_SIZE, FOX_HEADS, FOX_HD), 1.0),
        "cache_fox_logf": jax.nn.log_sigmoid(nrm(ks[4], (DEPTH, n_phys, PAGE_SIZE, FOX_HEADS), 1.0) + 4.0),
        "state_conv": nrm(ks[5], (DEPTH, DEC_BATCH, CONV_W - 1, D_CONV), 1.0),
        "state_ret": nrm(ks[6], (DEPTH, DEC_BATCH, RET_HEADS, RET_DK, RET_DV), 1.0),
        "page_table": page_table,
        "meta_tokens": nrm(ks[8], (N_META, D_MODEL), 1.0),
        "ln0_g": 1.0 + nrm(ks[9], (D_MODEL,), 0.02),
        "ln0_b": nrm(ks[10], (D_MODEL,), 0.02),
        "w_in": nrm(ks[11], (DEPTH, D_MODEL, N_IN), D_MODEL ** -0.5),
        "b_forget": jax.random.uniform(ks[12], (DEPTH, FOX_HEADS), jnp.float32, 1.0, 4.0),
        "conv_w": nrm(ks[13], (DEPTH, CONV_W, D_CONV), CONV_W ** -0.5),
        "conv_b": nrm(ks[14], (DEPTH, D_CONV), 0.02),
        "conv_ln_g": 1.0 + nrm(ks[15], (DEPTH, D_CONV), 0.02),
        "conv_ln_b": nrm(ks[16], (DEPTH, D_CONV), 0.02),
        "w_conv_out": nrm(ks[17], (DEPTH, D_CONV, D_MODEL), D_CONV ** -0.5 * BETA),
        "w_fox_out": nrm(ks[18], (DEPTH, D_FOX, D_MODEL), D_FOX ** -0.5 * BETA),
        "w_ret_out": nrm(ks[19], (DEPTH, D_RET_V, D_MODEL), D_RET_V ** -0.5 * BETA),
        "w_out": nrm(ks[20], (DEPTH, D_MODEL, D_MODEL), D_MODEL ** -0.5 * BETA),
        "ln1_g": 1.0 + nrm(ks[21], (DEPTH, D_MODEL), 0.02),
        "ln1_b": nrm(ks[22], (DEPTH, D_MODEL), 0.02),
        "w_ffn_gate": nrm(ks[23], (DEPTH, D_MODEL, D_FF), D_MODEL ** -0.5),
        "w_ffn_up": nrm(ks[24], (DEPTH, D_MODEL, D_FF), D_MODEL ** -0.5),
        "w_ffn_down": nrm(ks[25], (DEPTH, D_FF, D_MODEL), D_FF ** -0.5 * BETA),
        "ln2_g": 1.0 + nrm(ks[26], (DEPTH, D_MODEL), 0.02),
        "ln2_b": nrm(ks[27], (DEPTH, D_MODEL), 0.02),
    }


def reference(x_prompt, x_sample, cache_fox_k, cache_fox_v, cache_fox_logf, state_conv, state_ret, page_table,
              meta_tokens, ln0_g, ln0_b, w_in, b_forget, conv_w, conv_b, conv_ln_g, conv_ln_b,
              w_conv_out, w_fox_out, w_ret_out, w_out, ln1_g, ln1_b, w_ffn_gate, w_ffn_up, w_ffn_down, ln2_g, ln2_b):
    bp = x_prompt.shape[0]
    bs = x_sample.shape[0]
    meta = jnp.broadcast_to(meta_tokens[None].astype(x_prompt.dtype), (bp, N_META, D_MODEL))
    h_p = layer_norm(jnp.concatenate([meta, x_prompt], 1), ln0_g, ln0_b)
    h_s = layer_norm(x_sample, ln0_g, ln0_b)
    pos_p = jnp.arange(h_p.shape[1])
    pos_s = PAST_LEN + jnp.arange(h_s.shape[1])

    fk_p, fv_p, lf_p, cs_p, rs_p = [], [], [], [], []
    fk_s, fv_s, lf_s, cs_s, rs_s = [], [], [], [], []
    for l in range(DEPTH):
        ffn_args = (w_out[l], ln1_g[l], ln1_b[l], w_ffn_gate[l], w_ffn_up[l], w_ffn_down[l], ln2_g[l], ln2_b[l])
        glu, fq, fk, fv, lf, rq, rk, rv, rg, g_a, g_b, g_c = branch_inputs(h_p, w_in[l], b_forget[l], pos_p)
        T = h_p.shape[1]
        cpad = jnp.concatenate([jnp.zeros((bp, CONV_W - 1, D_CONV), glu.dtype), glu], 1)
        y_conv = conv_tail(cpad, conv_w[l], conv_b[l], conv_ln_g[l], conv_ln_b[l], w_conv_out[l])
        y_fox = fox_prompt(fq, fk, fv, lf).reshape(bp, T, D_FOX) @ w_fox_out[l]
        o_ret, S_p = retention_prompt(rq, rk, rv)
        y_ret = ret_tail(o_ret, rg, w_ret_out[l])
        h_p = merge_and_ffn(h_p, y_conv, y_fox, y_ret, g_a, g_b, g_c, *ffn_args)
        fk_p.append(fk)
        fv_p.append(fv)
        lf_p.append(lf.astype(cache_fox_logf.dtype))
        cs_p.append(cpad[:, -(CONV_W - 1):])
        rs_p.append(S_p.astype(state_ret.dtype))
        glu, fq, fk, fv, lf, rq, rk, rv, rg, g_a, g_b, g_c = branch_inputs(h_s, w_in[l], b_forget[l], pos_s)
        T = h_s.shape[1]
        cpad = jnp.concatenate([state_conv[l].astype(glu.dtype), glu], 1)
        y_conv = conv_tail(cpad, conv_w[l], conv_b[l], conv_ln_g[l], conv_ln_b[l], w_conv_out[l])
        k_past = cache_fox_k[l][page_table].reshape(bs, -1, FOX_HEADS, FOX_HD)
        v_past = cache_fox_v[l][page_table].reshape(bs, -1, FOX_HEADS, FOX_HD)
        lf_past = cache_fox_logf[l][page_table].reshape(bs, -1, FOX_HEADS)
        y_fox = fox_sample(fq, fk, fv, lf, k_past, v_past, lf_past).reshape(bs, T, D_FOX) @ w_fox_out[l]
        S_s, o_ret = ret_chunk(state_ret[l].astype(jnp.float32), rq, rk, rv)
        y_ret = ret_tail(o_ret, rg, w_ret_out[l])
        h_s = merge_and_ffn(h_s, y_conv, y_fox, y_ret, g_a, g_b, g_c, *ffn_args)
        fk_s.append(fk)
        fv_s.append(fv)
        lf_s.append(lf.astype(cache_fox_logf.dtype))
        cs_s.append(cpad[:, -(CONV_W - 1):])
        rs_s.append(S_s.astype(state_ret.dtype))

    y_prompt = h_p[:, N_META:]
    y_sample = h_s
    return (y_prompt, y_sample,
            jnp.stack(fk_p), jnp.stack(fv_p), jnp.stack(lf_p), jnp.stack(cs_p), jnp.stack(rs_p),
            jnp.stack(fk_s), jnp.stack(fv_s), jnp.stack(lf_s), jnp.stack(cs_s), jnp.stack(rs_s))
```

```python
import functools
import math

import jax
import jax.numpy as jnp
import numpy as np
from jax import lax
from jax.experimental import pallas as pl
from jax.experimental.pallas import tpu as pltpu

F32 = jnp.float32
BF16 = jnp.bfloat16

N_META = 16
HEAD_DIM = 128
Q_BLOCK = 128
RET_CHUNK = 128
ROPE_BASE = 10000.0
LN_EPS = 1e-5
NEG_BIG = -1e30
SAMPLE_ROWS = 16
COL_BLOCK = 512
GLU_HALF = COL_BLOCK // 2
LANES = 128
VMEM_LIMIT = 56 * 1024 * 1024


def _cparams(sem):
    return pltpu.CompilerParams(dimension_semantics=sem, vmem_limit_bytes=VMEM_LIMIT)


def _row_tile(rows_per_seq):
    best = 16
    for k in range(1, rows_per_seq // 16 + 1):
        if rows_per_seq % k == 0:
            t = rows_per_seq // k
            if t % 16 == 0 and t <= 1024:
                best = t
                break
    return best


def _ln(x, g, b):
    mu = jnp.mean(x, axis=-1, keepdims=True)
    xc = x - mu
    var = jnp.mean(xc * xc, axis=-1, keepdims=True)
    return xc * lax.rsqrt(var + LN_EPS) * g + b


def _sigmoid(x):
    return jax.nn.sigmoid(x)


def _log_sigmoid(x):
    return jnp.minimum(x, 0.0) - jnp.log1p(jnp.exp(-jnp.abs(x)))


def _dot(a, b):
    return jnp.dot(a, b, preferred_element_type=F32)


def _dot_nt(a, b):
    return lax.dot_general(a, b, (((1,), (1,)), ((), ())), preferred_element_type=F32)


def _dot_tn(a, b):
    return lax.dot_general(a, b, (((0,), (0,)), ((), ())), preferred_element_type=F32)


def _ln_kernel(x_ref, g_ref, b_ref, o_ref):
    o_ref[...] = _ln(x_ref[...], g_ref[...], b_ref[...])


def layer_norm_rows(x, g, b, tm):
    r, d = x.shape
    return pl.pallas_call(
        _ln_kernel,
        grid=(r // tm,),
        in_specs=[pl.BlockSpec((tm, d), lambda i: (i, 0)),
                  pl.BlockSpec((1, d), lambda i: (0, 0)),
                  pl.BlockSpec((1, d), lambda i: (0, 0))],
        out_specs=pl.BlockSpec((tm, d), lambda i: (i, 0)),
        out_shape=jax.ShapeDtypeStruct((r, d), F32),
        compiler_params=_cparams(("arbitrary",)),
    )(x, g.reshape(1, d), b.reshape(1, d))


def _inproj_kernel(h_ref, w_ref, wf_ref, bf_ref, cos_ref, sin_ref,
                   glu_ref, fq_ref, fk_ref, fv_ref, lf_ref, rq_ref, rk_ref, rv_ref, rg_ref, sg_ref,
                   xb_ref):
    j = pl.program_id(1)

    @pl.when(j == 0)
    def _():
        xb = h_ref[...].astype(BF16)
        xb_ref[...] = xb
        lf_ref[...] = _log_sigmoid(_dot(xb, wf_ref[...]) + bf_ref[...])

    z = _dot(xb_ref[...], w_ref[...])

    def rotary(scale):
        parts = []
        for hh in range(COL_BLOCK // HEAD_DIM):
            x = z[:, hh * HEAD_DIM:(hh + 1) * HEAD_DIM]
            xr = pltpu.roll(x, HEAD_DIM // 2, 1)
            parts.append((x * cos_ref[...] + xr * sin_ref[...]) * scale)
        return jnp.concatenate(parts, axis=1)

    @pl.when(j < 4)
    def _():
        glu_ref[...] = z[:, :GLU_HALF] * _sigmoid(z[:, GLU_HALF:])

    @pl.when((j >= 4) & (j < 6))
    def _():
        fq_ref[...] = (z * (HEAD_DIM ** -0.5)).astype(BF16)

    @pl.when((j >= 6) & (j < 8))
    def _():
        fk_ref[...] = z

    @pl.when((j >= 8) & (j < 10))
    def _():
        fv_ref[...] = z

    @pl.when((j >= 10) & (j < 12))
    def _():
        rq_ref[...] = rotary(1.0).astype(BF16)

    @pl.when((j >= 12) & (j < 14))
    def _():
        rk_ref[...] = rotary(HEAD_DIM ** -0.5).astype(BF16)

    @pl.when((j >= 14) & (j < 16))
    def _():
        rv_ref[...] = z.astype(BF16)

    @pl.when((j >= 16) & (j < 18))
    def _():
        rg_ref[...] = (z * _sigmoid(z)).astype(BF16)

    @pl.when(j >= 18)
    def _():
        sg_ref[...] = _sigmoid(z)


def in_projection(h, w_perm, wf, bf, cos_t, sin_t, layer, tm, tab_blocks):
    r, d = h.shape
    nb_gate = d // COL_BLOCK
    nblk = 18 + 3 * nb_gate
    inner = 1024

    def seg(first):
        return lambda i, j: (i, jnp.clip(j - first, 0, 1))

    out_shapes = [
        jax.ShapeDtypeStruct((r, inner), F32),
        jax.ShapeDtypeStruct((r, inner), BF16),
        jax.ShapeDtypeStruct((r, inner), F32),
        jax.ShapeDtypeStruct((r, inner), F32),
        jax.ShapeDtypeStruct((r, LANES), F32),
        jax.ShapeDtypeStruct((r, inner), BF16),
        jax.ShapeDtypeStruct((r, inner), BF16),
        jax.ShapeDtypeStruct((r, inner), BF16),
        jax.ShapeDtypeStruct((r, inner), BF16),
        jax.ShapeDtypeStruct((r, 3 * d), F32),
    ]
    out_specs = [
        pl.BlockSpec((tm, GLU_HALF), lambda i, j: (i, jnp.clip(j, 0, 3))),
        pl.BlockSpec((tm, COL_BLOCK), seg(4)),
        pl.BlockSpec((tm, COL_BLOCK), seg(6)),
        pl.BlockSpec((tm, COL_BLOCK), seg(8)),
        pl.BlockSpec((tm, LANES), lambda i, j: (i, 0)),
        pl.BlockSpec((tm, COL_BLOCK), seg(10)),
        pl.BlockSpec((tm, COL_BLOCK), seg(12)),
        pl.BlockSpec((tm, COL_BLOCK), seg(14)),
        pl.BlockSpec((tm, COL_BLOCK), seg(16)),
        pl.BlockSpec((tm, COL_BLOCK), lambda i, j: (i, jnp.clip(j - 18, 0, 3 * nb_gate - 1))),
    ]
    in_specs = [
        pl.BlockSpec((tm, d), lambda i, j: (i, 0)),
        pl.BlockSpec((None, d, COL_BLOCK), lambda i, j: (layer, 0, j)),
        pl.BlockSpec((None, d, LANES), lambda i, j: (layer, 0, 0)),
        pl.BlockSpec((None, 1, LANES), lambda i, j: (layer, 0, 0)),
        pl.BlockSpec((tm, HEAD_DIM), lambda i, j: (i % tab_blocks, 0)),
        pl.BlockSpec((tm, HEAD_DIM), lambda i, j: (i % tab_blocks, 0)),
    ]
    return pl.pallas_call(
        _inproj_kernel,
        grid=(r // tm, nblk),
        in_specs=in_specs,
        out_specs=out_specs,
        out_shape=out_shapes,
        scratch_shapes=[pltpu.VMEM((tm, d), BF16)],
        compiler_params=_cparams(("arbitrary", "arbitrary")),
    )(h, w_perm, wf, bf, cos_t, sin_t)


def _lfscan_kernel(n_rows, lf_ref, ccol_ref, crow_ref):
    ii = lax.broadcasted_iota(jnp.int32, (LANES, LANES), 0)
    jj = lax.broadcasted_iota(jnp.int32, (LANES, LANES), 1)
    tri = (ii >= jj).astype(F32)
    carry = jnp.zeros((1, LANES), F32)
    n_chunks = -(-n_rows // LANES)
    for c in range(n_chunks):
        lo = c * LANES
        n = min(LANES, n_rows - lo)
        x = lf_ref[0, lo:lo + n, :]
        if n < LANES:
            x = jnp.concatenate([x, jnp.zeros((LANES - n, LANES), F32)], axis=0)
        y = jnp.dot(tri, x, precision=lax.Precision.HIGHEST, preferred_element_type=F32) + carry
        ccol_ref[0, lo:lo + n, :] = y[:n]
        crow_ref[0, :, lo:lo + LANES] = y.T[:8, :]
        carry = y[n - 1:n, :]


def logf_prefix(lf):
    b, l, _ = lf.shape
    lpad = -(-l // LANES) * LANES
    return pl.pallas_call(
        functools.partial(_lfscan_kernel, l),
        grid=(b,),
        in_specs=[pl.BlockSpec((1, l, LANES), lambda i: (i, 0, 0))],
        out_specs=[pl.BlockSpec((1, l, LANES), lambda i: (i, 0, 0)),
                   pl.BlockSpec((1, 8, lpad), lambda i: (i, 0, 0))],
        out_shape=[jax.ShapeDtypeStruct((b, l, LANES), F32),
                   jax.ShapeDtypeStruct((b, 8, lpad), F32)],
        compiler_params=_cparams(("arbitrary",)),
    )(lf)


def _fox_prompt_kernel(n_rows, q_ref, k_ref, v_ref, ccol_ref, crow_ref, o_ref, kb_ref, vb_ref):
    h = pl.program_id(1)
    lpad = kb_ref.shape[0]
    kb_ref[0:n_rows, :] = k_ref[0].astype(BF16)
    vb_ref[0:n_rows, :] = v_ref[0].astype(BF16)
    if lpad > n_rows:
        kb_ref[n_rows:lpad, :] = jnp.zeros((lpad - n_rows, HEAD_DIM), BF16)
        vb_ref[n_rows:lpad, :] = jnp.zeros((lpad - n_rows, HEAD_DIM), BF16)
    lane = lax.broadcasted_iota(jnp.int32, (1, LANES), 1)
    crow = crow_ref[0, pl.ds(h, 1), :]
    blocks = [(i * Q_BLOCK, Q_BLOCK) for i in range(n_rows // Q_BLOCK)]
    if n_rows % Q_BLOCK:
        blocks.append((n_rows - n_rows % Q_BLOCK, n_rows % Q_BLOCK))
    for start, size in blocks:
        nk = -(-(start + size) // LANES) * LANES
        q = q_ref[0, start:start + size, :]
        cq = jnp.sum(jnp.where(lane == h, ccol_ref[0, start:start + size, :], 0.0), axis=1, keepdims=True)
        s = _dot_nt(q, kb_ref[0:nk, :]) + (cq - crow[:, :nk])
        qpos = start + lax.broadcasted_iota(jnp.int32, (size, nk), 0)
        kpos = lax.broadcasted_iota(jnp.int32, (size, nk), 1)
        s = jnp.where(qpos >= kpos, s, NEG_BIG)
        m = jnp.max(s, axis=1, keepdims=True)
        p = jnp.exp(s - m)
        den = jnp.sum(p, axis=1, keepdims=True)
        o = _dot(p.astype(BF16), vb_ref[0:nk, :]) / den
        o_ref[0, start:start + size, :] = o.astype(o_ref.dtype)


def fox_prompt(fq, fk, fv, ccol, crow):
    b, l, inner = fq.shape
    heads = inner // HEAD_DIM
    lpad = crow.shape[-1]
    hb = lambda i, j: (i, 0, j)
    return pl.pallas_call(
        functools.partial(_fox_prompt_kernel, l),
        grid=(b, heads),
        in_specs=[pl.BlockSpec((1, l, HEAD_DIM), hb),
                  pl.BlockSpec((1, l, HEAD_DIM), hb),
                  pl.BlockSpec((1, l, HEAD_DIM), hb),
                  pl.BlockSpec((1, l, LANES), lambda i, j: (i, 0, 0)),
                  pl.BlockSpec((1, 8, lpad), lambda i, j: (i, 0, 0))],
        out_specs=pl.BlockSpec((1, l, HEAD_DIM), hb),
        out_shape=jax.ShapeDtypeStruct((b, l, inner), BF16),
        scratch_shapes=[pltpu.VMEM((lpad, HEAD_DIM), BF16), pltpu.VMEM((lpad, HEAD_DIM), BF16)],
        compiler_params=_cparams(("arbitrary", "arbitrary")),
    )(fq, fk, fv, ccol, crow)


def _lane_scan(x, suffix):
    lane = lax.broadcasted_iota(jnp.int32, x.shape, 1)
    d = 1
    while d < LANES:
        if suffix:
            x = x + jnp.where(lane < LANES - d, pltpu.roll(x, LANES - d, 1), 0.0)
        else:
            x = x + jnp.where(lane >= d, pltpu.roll(x, d, 1), 0.0)
        d *= 2
    return x


def _fox_sample_kernel(n_new, heads, group, pt_ref, q_ref, kn_ref, vn_ref, lfn_ref, *rest):
    k_refs = rest[:group]
    v_refs = rest[group:2 * group]
    lf_refs = rest[2 * group:3 * group]
    o_ref = rest[3 * group]
    qblk_ref, m_ref, l_ref, acc_ref, carry_ref, pnew_ref, kpad_ref, vpad_ref = rest[3 * group + 1:]
    step = pl.program_id(1)
    rows = n_new * heads
    inner = heads * HEAD_DIM
    sub = lax.broadcasted_iota(jnp.int32, (heads, inner), 0)
    lane_head = lax.broadcasted_iota(jnp.int32, (heads, inner), 1) // HEAD_DIM
    hmask = sub == lane_head

    def tile_rows(x):
        return jnp.concatenate([x] * n_new, axis=0)

    def attend(kb, vb, bias, mask):
        s = _dot_nt(qblk_ref[...], kb) + bias
        if mask is not None:
            s = jnp.where(mask, s, NEG_BIG)
        m_old = m_ref[...]
        m_new = jnp.maximum(m_old, jnp.max(s, axis=1, keepdims=True))
        alpha = jnp.exp(m_old - m_new)
        p = jnp.exp(s - m_new[:, :1])
        l_ref[...] = alpha * l_ref[...] + jnp.sum(p, axis=1, keepdims=True)
        acc_ref[...] = alpha[:, :1] * acc_ref[...] + _dot(p.astype(BF16), vb)
        m_ref[...] = m_new

    @pl.when(step == 0)
    def _():
        q = q_ref[0].astype(F32)
        qblk_ref[...] = jnp.concatenate(
            [jnp.where(hmask, q[t:t + 1, :], 0.0) for t in range(n_new)], axis=0).astype(BF16)
        m_ref[...] = jnp.full(m_ref.shape, NEG_BIG, F32)
        l_ref[...] = jnp.zeros(l_ref.shape, F32)
        acc_ref[...] = jnp.zeros(acc_ref.shape, F32)
        carry_ref[...] = jnp.zeros(carry_ref.shape, F32)
        kpad_ref[...] = jnp.zeros(kpad_ref.shape, BF16)
        vpad_ref[...] = jnp.zeros(vpad_ref.shape, BF16)
        kpad_ref[0:SAMPLE_ROWS, :] = kn_ref[0].astype(BF16)
        vpad_ref[0:SAMPLE_ROWS, :] = vn_ref[0].astype(BF16)
        incl = _lane_scan(lfn_ref[0], suffix=False)
        pnew = jnp.concatenate(
            [jnp.broadcast_to(incl[:, t:t + 1], (heads, LANES)) for t in range(n_new)], axis=0)
        pnew_ref[...] = pnew
        col = lax.broadcasted_iota(jnp.int32, (rows, LANES), 1)
        tq = lax.broadcasted_iota(jnp.int32, (rows, LANES), 0) // heads
        attend(kpad_ref[...], vpad_ref[...], pnew - tile_rows(incl), col <= tq)

    for g in range(group):
        lf = lf_refs[g][...]
        incl = _lane_scan(lf, suffix=True)
        carry = carry_ref[...]
        bias = tile_rows(incl - lf + carry) + pnew_ref[...]
        attend(k_refs[g][...].astype(BF16), v_refs[g][...].astype(BF16), bias, None)
        carry_ref[...] = carry + jnp.broadcast_to(incl[:, 0:1], carry.shape)

    @pl.when(step == pl.num_programs(1) - 1)
    def _():
        out_rows = []
        for t in range(n_new):
            blk = acc_ref[t * heads:(t + 1) * heads, :] / l_ref[t * heads:(t + 1) * heads, 0:1]
            out_rows.append(jnp.sum(jnp.where(hmask, blk, 0.0), axis=0, keepdims=True))
        out_rows.append(jnp.zeros((SAMPLE_ROWS - n_new, inner), F32))
        o_ref[0] = jnp.concatenate(out_rows, axis=0).astype(o_ref.dtype)


def fox_sample(page_table, fq, fk, fv, lf_new_t, cache_k, cache_v, cache_lf_t, layer, n_new, group):
    db, _, inner = fq.shape
    heads = inner // HEAD_DIM
    n_pages = page_table.shape[1]
    page = cache_k.shape[2]
    assert page == LANES and n_pages % group == 0
    rows = n_new * heads

    def page_map(g):
        return lambda b, s, pt: (layer, pt[b, n_pages - 1 - (s * group + g)], 0, 0)

    seq = lambda b, s, pt: (b, 0, 0)
    in_specs = [pl.BlockSpec((1, SAMPLE_ROWS, inner), seq)] * 3 + [pl.BlockSpec((1, heads, LANES), seq)]
    in_specs += [pl.BlockSpec((None, None, page, inner), page_map(g)) for g in range(group)]
    in_specs += [pl.BlockSpec((None, None, page, inner), page_map(g)) for g in range(group)]
    in_specs += [pl.BlockSpec((None, None, heads, page), page_map(g)) for g in range(group)]
    grid_spec = pltpu.PrefetchScalarGridSpec(
        num_scalar_prefetch=1,
        grid=(db, n_pages // group),
        in_specs=in_specs,
        out_specs=pl.BlockSpec((1, SAMPLE_ROWS, inner), seq),
        scratch_shapes=[pltpu.VMEM((rows, inner), BF16),
                        pltpu.VMEM((rows, LANES), F32),
                        pltpu.VMEM((rows, LANES), F32),
                        pltpu.VMEM((rows, inner), F32),
                        pltpu.VMEM((heads, LANES), F32),
                        pltpu.VMEM((rows, LANES), F32),
                        pltpu.VMEM((page, inner), BF16),
                        pltpu.VMEM((page, inner), BF16)],
    )
    return pl.pallas_call(
        functools.partial(_fox_sample_kernel, n_new, heads, group),
        grid_spec=grid_spec,
        out_shape=jax.ShapeDtypeStruct((db, SAMPLE_ROWS, inner), BF16),
        compiler_params=_cparams(("arbitrary", "arbitrary")),
    )(page_table, fq, fk, fv, lf_new_t, *([cache_k] * group), *([cache_v] * group), *([cache_lf_t] * group))


def _retention_kernel(first_rows, first_len, n_chunks, q_ref, k_ref, v_ref, g_ref, s0_ref, lg_ref,
                      a_ref, s_ref):
    h = pl.program_id(1)
    lg = lg_ref[pl.ds(h, 1), :][:, :1]

    def chunk(state, q, k, v, length, rows):
        ii = lax.broadcasted_iota(jnp.int32, (rows, rows), 0)
        jj = lax.broadcasted_iota(jnp.int32, (rows, rows), 1)
        diff = (ii - jj).astype(F32)
        dmask = jnp.where(diff >= 0, jnp.exp(lg * jnp.maximum(diff, 0.0)), 0.0)
        idx = lax.broadcasted_iota(jnp.int32, (rows, 1), 0).astype(F32)
        scores = _dot_nt(q, k) * dmask
        inner = _dot(scores.astype(BF16), v)
        cross = _dot(q, state.astype(BF16)) * jnp.exp(lg * (idx + 1.0))
        kd = k.astype(F32) * jnp.exp(lg * (length - 1.0 - idx))
        kd = jnp.where(idx < length, kd, 0.0).astype(BF16)
        new_state = state * jnp.exp(lg * length) + _dot_tn(kd, v)
        return new_state, inner + cross

    def finish(o, gate):
        mu = jnp.mean(o, axis=-1, keepdims=True)
        oc = o - mu
        var = jnp.mean(oc * oc, axis=-1, keepdims=True)
        return (gate.astype(F32) * (oc * lax.rsqrt(var + LN_EPS))).astype(a_ref.dtype)

    state = s0_ref[0, 0]
    state, o = chunk(state, q_ref[0, 0:first_rows, :], k_ref[0, 0:first_rows, :], v_ref[0, 0:first_rows, :],
                     float(first_len), first_rows)
    a_ref[0, 0:first_rows, :] = finish(o, g_ref[0, 0:first_rows, :])

    def body(c, state):
        off = pl.multiple_of(first_rows + c * RET_CHUNK, 16)
        sl = pl.ds(off, RET_CHUNK)
        state, o = chunk(state, q_ref[0, sl, :], k_ref[0, sl, :], v_ref[0, sl, :], float(RET_CHUNK), RET_CHUNK)
        a_ref[0, sl, :] = finish(o, g_ref[0, sl, :])
        return state

    if n_chunks:
        state = lax.fori_loop(0, n_chunks, body, state)
    s_ref[0, 0] = state


def retention(rq, rk, rv, rgs, s0, lg_table, first_rows, first_len):
    b, t, inner = rq.shape
    heads = inner // HEAD_DIM
    n_chunks = (t - first_rows) // RET_CHUNK
    assert first_rows + n_chunks * RET_CHUNK == t
    hb = lambda i, j: (i, 0, j)
    st = lambda i, j: (i, j, 0, 0)
    return pl.pallas_call(
        functools.partial(_retention_kernel, first_rows, first_len, n_chunks),
        grid=(b, heads),
        in_specs=[pl.BlockSpec((1, t, HEAD_DIM), hb)] * 4
        + [pl.BlockSpec((1, 1, HEAD_DIM, HEAD_DIM), st),
           pl.BlockSpec((heads, LANES), lambda i, j: (0, 0))],
        out_specs=[pl.BlockSpec((1, t, HEAD_DIM), hb),
                   pl.BlockSpec((1, 1, HEAD_DIM, HEAD_DIM), st)],
        out_shape=[jax.ShapeDtypeStruct((b, t, inner), BF16),
                   jax.ShapeDtypeStruct((b, heads, HEAD_DIM, HEAD_DIM), F32)],
        compiler_params=_cparams(("arbitrary", "arbitrary")),
    )(rq, rk, rv, rgs, s0, lg_table)


CONV_PAD = 32
CONV_LANES = 256


def _conv_kernel(conv_w, n_valid, t_chunk, x_ref, st_ref, w_ref, b_ref, g_ref, beta_ref, a_ref, cs_ref, xp_ref):
    t = x_ref.shape[1]
    c = x_ref.shape[2]
    hist = conv_w - 1
    base = CONV_PAD - hist
    xp_ref[0:CONV_PAD, :] = jnp.zeros((CONV_PAD, c), F32)
    xp_ref[base:CONV_PAD, :] = st_ref[0]
    xp_ref[CONV_PAD:CONV_PAD + t, :] = x_ref[0]
    cs_ref[0] = xp_ref[base + n_valid:base + n_valid + hist, :]

    def body(i, carry):
        t0 = pl.multiple_of(i * t_chunk, 8)
        parts = []
        for cb in range(c // CONV_LANES):
            cols = slice(cb * CONV_LANES, (cb + 1) * CONV_LANES)
            win = xp_ref[pl.ds(t0, t_chunk + CONV_PAD), cols]
            acc = jnp.zeros((t_chunk, CONV_LANES), F32)
            for r in range(8):
                taps = [k for k in range(conv_w) if (base + k) % 8 == r]
                if not taps:
                    continue
                span = max((base + k) // 8 for k in taps) * 8 + t_chunk
                shifted = win[r:r + span, :]
                for k in taps:
                    m8 = (base + k) // 8 * 8
                    acc = acc + shifted[m8:m8 + t_chunk, :] * w_ref[k:k + 1, cols]
            parts.append(acc)
        y = jnp.concatenate(parts, axis=1) + b_ref[...]
        y = _ln(y, g_ref[...], beta_ref[...])
        a_ref[0, pl.ds(t0, t_chunk), :] = (y * _sigmoid(y)).astype(a_ref.dtype)
        return carry

    lax.fori_loop(0, t // t_chunk, body, 0)


def conv_branch(glu, state, w, bias, ln_g, ln_b, n_valid, t_chunk):
    b, t, c = glu.shape
    conv_w = w.shape[0]
    hist = conv_w - 1
    row = lambda i: (0, 0)
    seq = lambda i: (i, 0, 0)
    return pl.pallas_call(
        functools.partial(_conv_kernel, conv_w, n_valid, t_chunk),
        grid=(b,),
        in_specs=[pl.BlockSpec((1, t, c), seq), pl.BlockSpec((1, hist, c), seq),
                  pl.BlockSpec((conv_w, c), row), pl.BlockSpec((1, c), row),
                  pl.BlockSpec((1, c), row), pl.BlockSpec((1, c), row)],
        out_specs=[pl.BlockSpec((1, t, c), seq), pl.BlockSpec((1, hist, c), seq)],
        out_shape=[jax.ShapeDtypeStruct((b, t, c), BF16), jax.ShapeDtypeStruct((b, hist, c), F32)],
        scratch_shapes=[pltpu.VMEM((CONV_PAD + t, c), F32)],
        compiler_params=_cparams(("arbitrary",)),
    )(glu, state, w, bias.reshape(1, c), ln_g.reshape(1, c), ln_b.reshape(1, c))


def _merge_kernel(ac_ref, af_ref, ar_ref, wc_ref, wf_ref, wr_ref, ga_ref, gb_ref, gc_ref, o_ref):
    mix = ga_ref[...] * _dot(ac_ref[...], wc_ref[...])
    mix = mix + gb_ref[...] * _dot(af_ref[...], wf_ref[...])
    mix = mix + gc_ref[...] * _dot(ar_ref[...], wr_ref[...])
    o_ref[...] = mix.astype(o_ref.dtype)


def merge_projection(a_conv, a_fox, a_ret, wc, wf, wr, sg, layer, tm):
    r, inner = a_conv.shape
    d = wc.shape[-1]
    nb = d // COL_BLOCK
    act = pl.BlockSpec((tm, inner), lambda i, j: (i, 0))
    wsp = pl.BlockSpec((None, inner, COL_BLOCK), lambda i, j: (layer, 0, j))
    gates = [pl.BlockSpec((tm, COL_BLOCK), (lambda k: (lambda i, j: (i, k * nb + j)))(k)) for k in range(3)]
    return pl.pallas_call(
        _merge_kernel,
        grid=(r // tm, nb),
        in_specs=[act, act, act, wsp, wsp, wsp] + gates,
        out_specs=pl.BlockSpec((tm, COL_BLOCK), lambda i, j: (i, j)),
        out_shape=jax.ShapeDtypeStruct((r, d), BF16),
        compiler_params=_cparams(("arbitrary", "arbitrary")),
    )(a_conv, a_fox, a_ret, wc, wf, wr, sg, sg, sg)


def _wout_kernel(alpha, mix_ref, w_ref, x_ref, o_ref):
    o_ref[...] = alpha * x_ref[...] + _dot(mix_ref[...], w_ref[...])


def out_projection(mix, w_out, x, layer, alpha, tm):
    r, d = mix.shape
    nb = d // COL_BLOCK
    return pl.pallas_call(
        functools.partial(_wout_kernel, alpha),
        grid=(r // tm, nb),
        in_specs=[pl.BlockSpec((tm, d), lambda i, j: (i, 0)),
                  pl.BlockSpec((None, d, COL_BLOCK), lambda i, j: (layer, 0, j)),
                  pl.BlockSpec((tm, COL_BLOCK), lambda i, j: (i, j))],
        out_specs=pl.BlockSpec((tm, COL_BLOCK), lambda i, j: (i, j)),
        out_shape=jax.ShapeDtypeStruct((r, d), F32),
        compiler_params=_cparams(("arbitrary", "arbitrary")),
    )(mix, w_out, x)


def _ffn_kernel(alpha, pre_ref, g1_ref, b1_ref, wg_ref, wu_ref, wd_ref, g2_ref, b2_ref, o_ref, hb_ref, acc_ref):
    t = pl.program_id(1)

    @pl.when(t == 0)
    def _():
        hn = _ln(pre_ref[...], g1_ref[...], b1_ref[...])
        o_ref[...] = hn
        hb_ref[...] = hn.astype(BF16)
        acc_ref[...] = jnp.zeros(acc_ref.shape, F32)

    hb = hb_ref[...]
    gate = _dot(hb, wg_ref[...])
    up = _dot(hb, wu_ref[...])
    act = (gate * _sigmoid(gate) * up).astype(BF16)
    acc_ref[...] += _dot(act, wd_ref[...])

    @pl.when(t == pl.num_programs(1) - 1)
    def _():
        o_ref[...] = _ln(alpha * o_ref[...] + acc_ref[...], g2_ref[...], b2_ref[...])


def ffn_block(pre, ln1_g, ln1_b, w_gate, w_up, w_down, ln2_g, ln2_b, layer, alpha, tm):
    r, d = pre.shape
    dff = w_gate.shape[-1]
    nt = dff // COL_BLOCK
    vec = pl.BlockSpec((None, 1, d), lambda i, t: (layer, 0, 0))
    return pl.pallas_call(
        functools.partial(_ffn_kernel, alpha),
        grid=(r // tm, nt),
        in_specs=[pl.BlockSpec((tm, d), lambda i, t: (i, 0)), vec, vec,
                  pl.BlockSpec((None, d, COL_BLOCK), lambda i, t: (layer, 0, t)),
                  pl.BlockSpec((None, d, COL_BLOCK), lambda i, t: (layer, 0, t)),
                  pl.BlockSpec((None, COL_BLOCK, d), lambda i, t: (layer, t, 0)),
                  vec, vec],
        out_specs=pl.BlockSpec((tm, d), lambda i, t: (i, 0)),
        out_shape=jax.ShapeDtypeStruct((r, d), F32),
        scratch_shapes=[pltpu.VMEM((tm, d), BF16), pltpu.VMEM((tm, d), F32)],
        compiler_params=_cparams(("arbitrary", "arbitrary")),
    )(pre, ln1_g, ln1_b, w_gate, w_up, w_down, ln2_g, ln2_b)


def _rope_tables(pos):
    half = HEAD_DIM // 2
    inv = ROPE_BASE ** (-jnp.arange(half, dtype=F32) / half)
    ang = pos.astype(F32)[:, None] * inv[None, :]
    cos, sin = jnp.cos(ang), jnp.sin(ang)
    return jnp.concatenate([cos, cos], axis=1), jnp.concatenate([-sin, sin], axis=1)


def _permute_w_in(w_in, d_conv, inner, heads):
    cv, cg = w_in[..., :d_conv], w_in[..., d_conv:2 * d_conv]
    glu_blocks = []
    for q in range(d_conv // GLU_HALF):
        glu_blocks += [cv[..., q * GLU_HALF:(q + 1) * GLU_HALF], cg[..., q * GLU_HALF:(q + 1) * GLU_HALF]]
    f0 = 2 * d_conv
    fox = w_in[..., f0:f0 + 3 * inner]
    wf = w_in[..., f0 + 3 * inner:f0 + 3 * inner + heads]
    rest = w_in[..., f0 + 3 * inner + heads:]
    w_perm = jnp.concatenate(glu_blocks + [fox, rest], axis=-1).astype(BF16)
    wf = jnp.pad(wf, ((0, 0), (0, 0), (0, LANES - heads))).astype(BF16)
    return w_perm, wf


def kernel(x_prompt, x_sample, cache_fox_k, cache_fox_v, cache_fox_logf, state_conv, state_ret, page_table,
           meta_tokens, ln0_g, ln0_b, w_in, b_forget, conv_w, conv_b, conv_ln_g, conv_ln_b,
           w_conv_out, w_fox_out, w_ret_out, w_out, ln1_g, ln1_b, w_ffn_gate, w_ffn_up, w_ffn_down, ln2_g, ln2_b):
    bp, seq, d = x_prompt.shape
    bs, n_new, _ = x_sample.shape
    depth, n_phys, page, heads, hd = cache_fox_k.shape
    assert hd == HEAD_DIM and n_new <= SAMPLE_ROWS
    inner = heads * hd
    d_conv = conv_w.shape[-1]
    assert d_conv == inner == 4 * GLU_HALF and d % COL_BLOCK == 0
    conv_hist = conv_w.shape[1] - 1
    n_pages = page_table.shape[1]
    past_len = n_pages * page
    lp = seq + N_META
    alpha = (2.0 * depth) ** 0.25

    w_perm, wf = _permute_w_in(w_in, d_conv, inner, heads)
    bf = jnp.pad(b_forget, ((0, 0), (0, LANES - heads))).reshape(depth, 1, LANES)
    wc_b, wfo_b, wr_b = w_conv_out.astype(BF16), w_fox_out.astype(BF16), w_ret_out.astype(BF16)
    wo_b = w_out.astype(BF16)
    wg_b, wu_b, wd_b = w_ffn_gate.astype(BF16), w_ffn_up.astype(BF16), w_ffn_down.astype(BF16)
    vec = lambda v: v.reshape(depth, 1, -1)
    ln1g, ln1b, ln2g, ln2b = vec(ln1_g), vec(ln1_b), vec(ln2_g), vec(ln2_b)

    cos_p, sin_p = _rope_tables(jnp.arange(lp))
    cos_s, sin_s = _rope_tables(jnp.tile(past_len + jnp.arange(SAMPLE_ROWS), bs))
    lg = jnp.log(1.0 - jnp.power(2.0, -5.0 - jnp.arange(heads, dtype=F32)))
    lg_table = jnp.broadcast_to(lg[:, None], (heads, LANES))

    cache_k = cache_fox_k.reshape(depth, n_phys, page, inner)
    cache_v = cache_fox_v.reshape(depth, n_phys, page, inner)
    cache_lf_t = jnp.swapaxes(cache_fox_logf, 2, 3)

    tm_p = _row_tile(lp)
    tm_s = bs * SAMPLE_ROWS
    group = 4 if n_pages % 4 == 0 else 1
    t_chunk_p = 48 if lp % 48 == 0 else 16

    meta = jnp.broadcast_to(meta_tokens[None].astype(x_prompt.dtype), (bp, N_META, d))
    xp_rows = jnp.concatenate([meta, x_prompt], axis=1).reshape(bp * lp, d)
    xs_rows = jnp.pad(x_sample, ((0, 0), (0, SAMPLE_ROWS - n_new), (0, 0))).reshape(bs * SAMPLE_ROWS, d)
    h_p = layer_norm_rows(xp_rows, ln0_g, ln0_b, tm_p)
    h_s = layer_norm_rows(xs_rows, ln0_g, ln0_b, tm_s)

    zeros_conv = jnp.zeros((bp, conv_hist, d_conv), F32)
    zeros_ret = jnp.zeros((bp, heads, hd, hd), F32)

    outs = {k: [] for k in ("fk_p", "fv_p", "lf_p", "cs_p", "rs_p", "fk_s", "fv_s", "lf_s", "cs_s", "rs_s")}
    for l in range(depth):
        glu, fq, fk, fv, lf, rq, rk, rv, rgs, sg = in_projection(
            h_p, w_perm, wf, bf, cos_p, sin_p, l, tm_p, lp // tm_p)
        s3 = lambda a: a.reshape(bp, lp, a.shape[-1])
        a_conv, cs = conv_branch(s3(glu), zeros_conv, conv_w[l], conv_b[l], conv_ln_g[l], conv_ln_b[l], lp, t_chunk_p)
        ccol, crow = logf_prefix(s3(lf))
        a_fox = fox_prompt(s3(fq), s3(fk), s3(fv), ccol, crow)
        a_ret, rs = retention(s3(rq), s3(rk), s3(rv), s3(rgs), zeros_ret, lg_table, N_META, N_META)
        flat = lambda a: a.reshape(bp * lp, a.shape[-1])
        mix = merge_projection(flat(a_conv), flat(a_fox), flat(a_ret), wc_b, wfo_b, wr_b, sg, l, tm_p)
        pre = out_projection(mix, wo_b, h_p, l, alpha, tm_p)
        h_p = ffn_block(pre, ln1g, ln1b, wg_b, wu_b, wd_b, ln2g, ln2b, l, alpha, tm_p)
        outs["fk_p"].append(fk.reshape(bp, lp, heads, hd))
        outs["fv_p"].append(fv.reshape(bp, lp, heads, hd))
        outs["lf_p"].append(s3(lf)[:, :, :heads])
        outs["cs_p"].append(cs)
        outs["rs_p"].append(rs)

        glu, fq, fk, fv, lf, rq, rk, rv, rgs, sg = in_projection(
            h_s, w_perm, wf, bf, cos_s, sin_s, l, tm_s, 1)
        t3 = lambda a: a.reshape(bs, SAMPLE_ROWS, a.shape[-1])
        a_conv, cs = conv_branch(t3(glu), state_conv[l], conv_w[l], conv_b[l], conv_ln_g[l], conv_ln_b[l],
                                 n_new, SAMPLE_ROWS)
        lf3 = t3(lf)
        lf_new_t = jnp.pad(jnp.swapaxes(lf3[:, :n_new, :heads], 1, 2), ((0, 0), (0, 0), (0, LANES - n_new)))
        a_fox = fox_sample(page_table, t3(fq), t3(fk), t3(fv), lf_new_t, cache_k, cache_v, cache_lf_t, l,
                           n_new, group)
        a_ret, rs = retention(t3(rq), t3(rk), t3(rv), t3(rgs), state_ret[l], lg_table, SAMPLE_ROWS, n_new)
        flat = lambda a: a.reshape(bs * SAMPLE_ROWS, a.shape[-1])
        mix = merge_projection(flat(a_conv), flat(a_fox), flat(a_ret), wc_b, wfo_b, wr_b, sg, l, tm_s)
        pre = out_projection(mix, wo_b, h_s, l, alpha, tm_s)
        h_s = ffn_block(pre, ln1g, ln1b, wg_b, wu_b, wd_b, ln2g, ln2b, l, alpha, tm_s)
        outs["fk_s"].append(t3(fk)[:, :n_new].reshape(bs, n_new, heads, hd))
        outs["fv_s"].append(t3(fv)[:, :n_new].reshape(bs, n_new, heads, hd))
        outs["lf_s"].append(lf3[:, :n_new, :heads])
        outs["cs_s"].append(cs)
        outs["rs_s"].append(rs)

    y_prompt = h_p.reshape(bp, lp, d)[:, N_META:]
    y_sample = h_s.reshape(bs, SAMPLE_ROWS, d)[:, :n_new]
    st = lambda k: jnp.stack(outs[k])
    return (y_prompt, y_sample,
            st("fk_p"), st("fv_p"), st("lf_p"), st("cs_p"), st("rs_p"),
            st("fk_s"), st("fv_s"), st("lf_s"), st("cs_s"), st("rs_s"))
```
